```python
import math
import jax, jax.numpy as jnp
from jax import lax
import numpy as np

D_MODEL = 1024
BATCH = 16
SEQ = 4096
DEPTH = 4

N_MIXERS = 2
N_ATTN_LAYERS = (DEPTH + 1) // 2
N_CONV_LAYERS = DEPTH // 2
N_HEADS = 8
HEAD_DIM = D_MODEL // N_HEADS // 2
V_DIM = 2 * HEAD_DIM
Q_BLOCK = 128
NUM_BUCKETS = 32
MAX_DISTANCE = 128
CONV_WIDTH = 3
N_EXPERTS = 32
TOP_K = 4
D_EXPERT = D_MODEL
SWIGLU_ALPHA = 1.702
SWIGLU_LIMIT = 7.0
MOE_BLOCK = 512
DEEPNORM_ALPHA = (2.0 * DEPTH) ** 0.25
DEEPNORM_BETA = (8.0 * DEPTH) ** -0.25
LN_EPS = 1e-5

kernel_name = "hybrid_diffattn_shortconv_moe_encoder"


def layer_norm(x, g, b):
    xf = x.astype(jnp.float32)
    mu = jnp.mean(xf, axis=-1, keepdims=True)
    var = jnp.mean(jnp.square(xf - mu), axis=-1, keepdims=True)
    y = (xf - mu) * lax.rsqrt(var + LN_EPS) * g.astype(jnp.float32) + b.astype(jnp.float32)
    return y.astype(x.dtype)


def rms_norm(x, g):
    xf = x.astype(jnp.float32)
    y = xf * lax.rsqrt(jnp.mean(jnp.square(xf), axis=-1, keepdims=True) + LN_EPS) * g.astype(jnp.float32)
    return y.astype(x.dtype)


def rel_bucket(rel):
    nb = NUM_BUCKETS // 2
    max_exact = nb // 2
    ret = jnp.where(rel > 0, nb, 0)
    n = jnp.abs(rel)
    nf = jnp.maximum(n, 1).astype(jnp.float32)
    large = max_exact + (jnp.log(nf / max_exact) / math.log(MAX_DISTANCE / max_exact)
                         * (nb - max_exact)).astype(jnp.int32)
    large = jnp.minimum(large, nb - 1)
    return ret + jnp.where(n < max_exact, n, large)


def diff_attention(x, w_in, w_out, lam_p, subln_g, rel_bias, lambda_init):
    B, S, D = x.shape
    qkv = x @ w_in
    q, k, v = jnp.split(qkv, 3, axis=-1)
    q = q.reshape(B, S, N_HEADS, 2, HEAD_DIM) * (HEAD_DIM ** -0.5)
    k = k.reshape(B, S, N_HEADS, 2, HEAD_DIM)
    v = v.reshape(B, S, N_HEADS, V_DIM)
    lf = lam_p.astype(jnp.float32)
    lam = jnp.exp(jnp.sum(lf[0] * lf[1])) - jnp.exp(jnp.sum(lf[2] * lf[3])) + lambda_init
    nqb = S // Q_BLOCK
    qb = q.reshape(B, nqb, Q_BLOCK, N_HEADS, 2, HEAD_DIM).transpose(1, 0, 2, 3, 4, 5)
    k_pos = jnp.arange(S, dtype=jnp.int32)

    def block(args):
        qblk, start = args
        q_pos = start + jnp.arange(Q_BLOCK, dtype=jnp.int32)
        bucket = rel_bucket(k_pos[None, :] - q_pos[:, None])
        bias = rel_bias[bucket].astype(jnp.float32).transpose(2, 0, 1)
        s = jnp.einsum('bqhmd,bkhmd->bmhqk', qblk, k).astype(jnp.float32) + bias
        a = jax.nn.softmax(s, axis=-1)
        diff = a[:, 0] - lam * a[:, 1]
        return jnp.einsum('bhqk,bkhv->bqhv', diff.astype(v.dtype), v)

    o = lax.map(block, (qb, jnp.arange(nqb, dtype=jnp.int32) * Q_BLOCK))
    o = o.transpose(1, 0, 2, 3, 4).reshape(B, S, N_HEADS, V_DIM)
    o = rms_norm(o, subln_g) * (1.0 - lambda_init)
    return o.reshape(B, S, N_HEADS * V_DIM) @ w_out


def short_conv(x, w_in, conv_w, w_out):
    B, S, D = x.shape
    h = x @ w_in
    gate_b, gate_c, u = jnp.split(h, 3, axis=-1)
    z = gate_c * u
    zc = lax.conv_general_dilated(z, conv_w.reshape(CONV_WIDTH, 1, D).astype(z.dtype),
                                  window_strides=(1,), padding=((CONV_WIDTH // 2, CONV_WIDTH // 2),),
                                  dimension_numbers=('NWC', 'WIO', 'NWC'),
                                  feature_group_count=D)
    return (gate_b * zc) @ w_out


def clamped_swiglu(h):
    glu = jnp.minimum(h[..., ::2], SWIGLU_LIMIT)
    lin = jnp.clip(h[..., 1::2], -SWIGLU_LIMIT, SWIGLU_LIMIT)
    return glu * jax.nn.sigmoid(SWIGLU_ALPHA * glu) * (lin + 1.0)


def moe(x2d, router_w, router_b, w1, b1, w2, b2):
    N, D = x2d.shape
    NK = N * TOP_K
    logits = (x2d @ router_w).astype(jnp.float32) + router_b.astype(jnp.float32)
    top_v, top_e = lax.top_k(logits, TOP_K)
    gates = jax.nn.softmax(top_v, axis=-1)
    flat_e = top_e.reshape(-1).astype(jnp.int32)
    flat_tok = jnp.arange(NK, dtype=jnp.int32) // TOP_K
    flat_g = gates.reshape(-1)
    order = jnp.argsort(flat_e)
    se, st, sg = flat_e[order], flat_tok[order], flat_g[order]
    counts = jnp.bincount(flat_e, length=N_EXPERTS).astype(jnp.int32)
    padded = (counts + MOE_BLOCK - 1) // MOE_BLOCK * MOE_BLOCK
    g_start = jnp.cumsum(counts) - counts
    p_end = jnp.cumsum(padded)
    p_start = p_end - padded
    dest = p_start[se] + jnp.arange(NK, dtype=jnp.int32) - g_start[se]
    n_blocks = (NK + N_EXPERTS * (MOE_BLOCK - 1) + MOE_BLOCK - 1) // MOE_BLOCK
    P = n_blocks * MOE_BLOCK
    row_tok = jnp.zeros((P,), jnp.int32).at[dest].set(st)
    row_gate = jnp.zeros((P,), jnp.float32).at[dest].set(sg)
    blk_e = jnp.minimum(jnp.searchsorted(p_end, jnp.arange(n_blocks, dtype=jnp.int32) * MOE_BLOCK,
                                         side='right'), N_EXPERTS - 1)

    def expert_block(args):
        tok, g, e = args
        h = x2d[tok] @ w1[e] + b1[e]
        y = clamped_swiglu(h) @ w2[e] + b2[e]
        return y * g[:, None].astype(y.dtype)

    rows = lax.map(expert_block, (row_tok.reshape(n_blocks, MOE_BLOCK),
                                  row_gate.reshape(n_blocks, MOE_BLOCK), blk_e))
    return jnp.zeros_like(x2d).at[row_tok].add(rows.reshape(P, D).astype(x2d.dtype))


def setup_inputs(seed: int = 0) -> dict:
    key = jax.random.key(seed)
    ks = jax.random.split(key, 20)
    D, E, F = D_MODEL, N_EXPERTS, D_EXPERT
    nrm = jax.random.normal
    return {
        "x": nrm(ks[0], (BATCH, SEQ, D), jnp.float32),
        "rel_bias": 0.5 * nrm(ks[1], (NUM_BUCKETS, N_HEADS), jnp.float32),
        "attn_w_in": nrm(ks[2], (N_ATTN_LAYERS, D, 3 * D), jnp.float32) * D ** -0.5,
        "attn_lambda": 0.1 * nrm(ks[3], (N_ATTN_LAYERS, 4, HEAD_DIM), jnp.float32),
        "attn_subln": 1.0 + 0.01 * nrm(ks[4], (N_ATTN_LAYERS, V_DIM), jnp.float32),
        "attn_w_out": nrm(ks[5], (N_ATTN_LAYERS, N_HEADS * V_DIM, D), jnp.float32) * (N_HEADS * V_DIM) ** -0.5 * DEEPNORM_BETA,
        "conv_w_in": nrm(ks[6], (N_CONV_LAYERS, D, 3 * D), jnp.float32) * D ** -0.5,
        "conv_w": nrm(ks[7], (N_CONV_LAYERS, CONV_WIDTH, D), jnp.float32) * CONV_WIDTH ** -0.5,
        "conv_w_out": nrm(ks[8], (N_CONV_LAYERS, D, D), jnp.float32) * D ** -0.5 * DEEPNORM_BETA,
        "router_w": nrm(ks[9], (DEPTH, D, E), jnp.float32) * D ** -0.5,
        "router_b": 0.01 * nrm(ks[10], (DEPTH, E), jnp.float32),
        "w1": nrm(ks[11], (DEPTH, E, D, 2 * F), jnp.float32) * D ** -0.5,
        "b1": 0.01 * nrm(ks[12], (DEPTH, E, 2 * F), jnp.float32),
        "w2": nrm(ks[13], (DEPTH, E, F, D), jnp.float32) * F ** -0.5 * DEEPNORM_BETA,
        "b2": 0.01 * nrm(ks[14], (DEPTH, E, D), jnp.float32),
        "ln_g": 1.0 + 0.01 * nrm(ks[15], (DEPTH, 2, D), jnp.float32),
        "ln_b": 0.01 * nrm(ks[16], (DEPTH, 2, D), jnp.float32),
    }


def reference(x, rel_bias, attn_w_in, attn_lambda, attn_subln, attn_w_out,
              conv_w_in, conv_w, conv_w_out, router_w, router_b, w1, b1, w2, b2,
              ln_g, ln_b):
    B, S, D = x.shape
    for i in range(DEPTH):
        j = i // N_MIXERS
        if i % N_MIXERS == 0:
            lambda_init = 0.8 - 0.6 * math.exp(-0.3 * i)
            h = diff_attention(x, attn_w_in[j], attn_w_out[j], attn_lambda[j], attn_subln[j],
                               rel_bias, lambda_init)
        else:
            h = short_conv(x, conv_w_in[j], conv_w[j], conv_w_out[j])
        x = layer_norm(DEEPNORM_ALPHA * x + h, ln_g[i, 0], ln_b[i, 0])
        m = moe(x.reshape(B * S, D), router_w[i], router_b[i], w1[i], b1[i], w2[i], b2[i])
        x = layer_norm(DEEPNORM_ALPHA * x + m.reshape(B, S, D), ln_g[i, 1], ln_b[i, 1])
    return x
```

```python
import functools
import math

import jax
import jax.numpy as jnp
from jax import lax
from jax.experimental import pallas as pl
from jax.experimental.pallas import tpu as pltpu

N_HEADS = 8
HEAD_DIM = 64
V_DIM = 2 * HEAD_DIM
NUM_BUCKETS = 32
MAX_DISTANCE = 128
CONV_WIDTH = 3
N_EXPERTS = 32
TOP_K = 4
SWIGLU_ALPHA = 1.702
SWIGLU_LIMIT = 7.0
LN_EPS = 1e-5
LOG2E = 1.4426950408889634
NEG_BIG = -1e30
LANES = 128
SCATTER_UNROLL = 8
VMEM_LIMIT = 56 * 1024 * 1024

F32 = jnp.float32
BF16 = jnp.bfloat16


def _cparams(sem, vmem=VMEM_LIMIT):
    return pltpu.CompilerParams(dimension_semantics=sem, vmem_limit_bytes=vmem)


def _layer_norm(t, g, b):
    mu = jnp.mean(t, axis=-1, keepdims=True)
    c = t - mu
    var = jnp.mean(c * c, axis=-1, keepdims=True)
    return c * lax.rsqrt(var + LN_EPS) * g + b


def _router_topk(x1, rw, rb):
    logits = jnp.dot(x1, rw, preferred_element_type=F32, precision=lax.Precision.HIGHEST) + rb
    tm = logits.shape[0]
    lane = lax.broadcasted_iota(jnp.int32, (tm, LANES), 1)
    vals, idxs = [], []
    lg = logits
    for _ in range(TOP_K):
        mx = jnp.max(lg, axis=-1, keepdims=True)
        idx = jnp.min(jnp.where(lg == mx, lane, LANES), axis=-1, keepdims=True)
        vals.append(mx)
        idxs.append(idx)
        lg = jnp.where(lane == idx, -jnp.inf, lg)
    es = [jnp.exp(v - vals[0]) for v in vals]
    denom = es[0] + es[1] + es[2] + es[3]
    gates = jnp.zeros((tm, LANES), F32)
    experts = jnp.zeros((tm, LANES), jnp.int32)
    for k in range(TOP_K):
        gates = jnp.where(lane == k, es[k] / denom, gates)
        experts = jnp.where(lane == k, idxs[k], experts)
    return gates, experts


def _ln_router_epilogue(pre, lng_ref, lnb_ref, rw_ref, rb_ref, x1_ref, gate_ref, exp_ref):
    x1 = _layer_norm(pre, lng_ref[...], lnb_ref[...])
    x1_ref[...] = x1
    gates, experts = _router_topk(x1, rw_ref[...], rb_ref[...])
    gate_ref[...] = gates
    exp_ref[...] = experts


def _attn_proj_kernel(x_ref, wq_ref, wk_ref, wvt_ref, q_ref, k_ref, vt_ref, *, q_scale):
    xb = x_ref[...].astype(BF16)
    q = jnp.dot(xb, wq_ref[...], preferred_element_type=F32) * q_scale
    q_ref[...] = q.astype(BF16)
    k_ref[...] = jnp.dot(xb, wk_ref[...], preferred_element_type=F32).astype(BF16)
    vt = lax.dot_general(wvt_ref[...], xb, (((1,), (1,)), ((), ())), preferred_element_type=F32)
    vt_ref[0] = vt.astype(BF16)


def _attn_proj(x2d, wq, wk, wvt, B, S, tm):
    N, D = x2d.shape
    ns = S // tm
    row = lambda b, i: (b * ns + i, 0)
    full = lambda b, i: (0, 0)
    return pl.pallas_call(
        functools.partial(_attn_proj_kernel, q_scale=LOG2E * HEAD_DIM ** -0.5),
        grid=(B, ns),
        in_specs=[pl.BlockSpec((tm, D), row),
                  pl.BlockSpec((D, D), full), pl.BlockSpec((D, D), full), pl.BlockSpec((D, D), full)],
        out_specs=[pl.BlockSpec((tm, D), row), pl.BlockSpec((tm, D), row),
                   pl.BlockSpec((1, D, tm), lambda b, i: (b, 0, i))],
        out_shape=[jax.ShapeDtypeStruct((N, D), BF16), jax.ShapeDtypeStruct((N, D), BF16),
                   jax.ShapeDtypeStruct((B, D, S), BF16)],
        compiler_params=_cparams(("parallel", "parallel")),
        name="attn_proj",
    )(x2d, wq, wk, wvt)


def _flash_kernel(lam_ref, q_ref, k_ref, vt_ref, bias_ref, g_ref, o_ref,
                  m_ref, l_ref, acc_ref, *, nk):
    i = pl.program_id(2)
    j = pl.program_id(3)

    @pl.when(j == 0)
    def _():
        m_ref[...] = jnp.full_like(m_ref, NEG_BIG)
        l_ref[...] = jnp.zeros_like(l_ref)
        acc_ref[...] = jnp.zeros_like(acc_ref)

    d = jnp.clip(j - i, -2, 2) + 2
    bias = bias_ref[0, d]
    q = q_ref[...]
    k = k_ref[...]
    vt = vt_ref[0]
    for mp in range(2):
        km = k[:, mp * HEAD_DIM:(mp + 1) * HEAD_DIM]
        qm = q[:, mp * HEAD_DIM:(mp + 1) * HEAD_DIM]
        s = lax.dot_general(km, qm, (((1,), (1,)), ((), ())), preferred_element_type=F32) + bias
        m_old = m_ref[mp]
        m_new = jnp.maximum(m_old, jnp.max(s, axis=0, keepdims=True))
        alpha = jnp.exp2(m_old - m_new)
        p = jnp.exp2(s - m_new)
        l_ref[mp] = alpha * l_ref[mp] + jnp.sum(p, axis=0, keepdims=True)
        acc_ref[mp] = alpha * acc_ref[mp] + jnp.dot(vt, p.astype(BF16), preferred_element_type=F32)
        m_ref[mp] = m_new

    @pl.when(j == nk - 1)
    def _():
        lam = lam_ref[0]
        o = acc_ref[0] / l_ref[0] - lam * (acc_ref[1] / l_ref[1])
        ms = jnp.mean(o * o, axis=0, keepdims=True)
        o = o * lax.rsqrt(ms + LN_EPS)
        o_ref[...] = (o.T * g_ref[...]).astype(BF16)


def _flash_attention(lam, q, k, vt, bias_tiles, g, B, S, T):
    N, D = q.shape
    nq = S // T
    nk = S // T
    grid_spec = pltpu.PrefetchScalarGridSpec(
        num_scalar_prefetch=1,
        grid=(N_HEADS, B, nq, nk),
        in_specs=[
            pl.BlockSpec((T, V_DIM), lambda h, b, i, j, lam: (b * nq + i, h)),
            pl.BlockSpec((T, V_DIM), lambda h, b, i, j, lam: (b * nk + j, h)),
            pl.BlockSpec((1, V_DIM, T), lambda h, b, i, j, lam: (b, h, j)),
            pl.BlockSpec((1, 5, T, T), lambda h, b, i, j, lam: (h, 0, 0, 0)),
            pl.BlockSpec((1, V_DIM), lambda h, b, i, j, lam: (0, 0)),
        ],
        out_specs=pl.BlockSpec((T, V_DIM), lambda h, b, i, j, lam: (b * nq + i, h)),
        scratch_shapes=[pltpu.VMEM((2, 1, T), F32), pltpu.VMEM((2, 1, T), F32),
                        pltpu.VMEM((2, V_DIM, T), F32)],
    )
    return pl.pallas_call(
        functools.partial(_flash_kernel, nk=nk),
        grid_spec=grid_spec,
        out_shape=jax.ShapeDtypeStruct((N, D), BF16),
        compiler_params=_cparams(("parallel", "parallel", "parallel", "arbitrary")),
        name="flash_diff_attn",
    )(lam, q, k, vt, bias_tiles, g)


def _outproj_kernel(o_ref, w_ref, x_ref, lng_ref, lnb_ref, rw_ref, rb_ref,
                    x1_ref, gate_ref, exp_ref, *, alpha):
    h = jnp.dot(o_ref[...], w_ref[...], preferred_element_type=F32)
    pre = alpha * x_ref[...] + h
    _ln_router_epilogue(pre, lng_ref, lnb_ref, rw_ref, rb_ref, x1_ref, gate_ref, exp_ref)


def _outproj_ln_router(o, w_out, x2d, lng, lnb, rw, rb, alpha, tm):
    N, D = x2d.shape
    row = lambda i: (i, 0)
    full = lambda i: (0, 0)
    return pl.pallas_call(
        functools.partial(_outproj_kernel, alpha=alpha),
        grid=(N // tm,),
        in_specs=[pl.BlockSpec((tm, D), row), pl.BlockSpec((D, D), full), pl.BlockSpec((tm, D), row),
                  pl.BlockSpec((1, D), full), pl.BlockSpec((1, D), full),
                  pl.BlockSpec((D, LANES), full), pl.BlockSpec((1, LANES), full)],
        out_specs=[pl.BlockSpec((tm, D), row), pl.BlockSpec((tm, LANES), row), pl.BlockSpec((tm, LANES), row)],
        out_shape=[jax.ShapeDtypeStruct((N, D), F32), jax.ShapeDtypeStruct((N, LANES), F32),
                   jax.ShapeDtypeStruct((N, LANES), jnp.int32)],
        compiler_params=_cparams(("parallel",)),
        name="outproj_ln_router",
    )(o, w_out, x2d, lng, lnb, rw, rb)


def _conv_kernel(x_ref, xp_ref, xn_ref, win_ref, cw_ref, wout_ref, lng_ref, lnb_ref, rw_ref, rb_ref,
                 x1_ref, gate_ref, exp_ref, *, alpha, ns):
    i = pl.program_id(1)
    x = x_ref[...]
    tm, D = x.shape
    h = jnp.dot(x.astype(BF16), win_ref[...], preferred_element_type=F32)
    gb = h[:, :D]
    z = h[:, D:2 * D] * h[:, 2 * D:]
    xh = jnp.concatenate([xp_ref[...], xn_ref[...]], axis=0).astype(BF16)
    hh = jnp.dot(xh, win_ref[:, D:], preferred_element_type=F32)
    zh = hh[:, :D] * hh[:, D:]
    z_before = jnp.where(i > 0, zh[7:8, :], 0.0)
    z_after = jnp.where(i < ns - 1, zh[8:9, :], 0.0)
    rowid = lax.broadcasted_iota(jnp.int32, (tm, 1), 0)
    z_m = jnp.where(rowid == 0, z_before, pltpu.roll(z, 1, 0))
    z_p = jnp.where(rowid == tm - 1, z_after, pltpu.roll(z, tm - 1, 0))
    cw = cw_ref[...]
    zc = cw[0:1, :] * z_m + cw[1:2, :] * z + cw[2:3, :] * z_p
    gin = (gb * zc).astype(BF16)
    pre = alpha * x + jnp.dot(gin, wout_ref[...], preferred_element_type=F32)
    _ln_router_epilogue(pre, lng_ref, lnb_ref, rw_ref, rb_ref, x1_ref, gate_ref, exp_ref)


def _conv_layer(x2d, w_in, conv_w, w_out, lng, lnb, rw, rb, alpha, B, S, tm):
    N, D = x2d.shape
    ns = S // tm
    hb = tm // 8
    nb8 = N // 8
    row = lambda b, i: (b * ns + i, 0)
    full = lambda b, i: (0, 0)
    prev8 = lambda b, i: (jnp.maximum((b * ns + i) * hb - 1, 0), 0)
    next8 = lambda b, i: (jnp.minimum((b * ns + i + 1) * hb, nb8 - 1), 0)
    return pl.pallas_call(
        functools.partial(_conv_kernel, alpha=alpha, ns=ns),
        grid=(B, ns),
        in_specs=[pl.BlockSpec((tm, D), row), pl.BlockSpec((8, D), prev8), pl.BlockSpec((8, D), next8),
                  pl.BlockSpec((D, 3 * D), full), pl.BlockSpec((CONV_WIDTH, D), full),
                  pl.BlockSpec((D, D), full), pl.BlockSpec((1, D), full), pl.BlockSpec((1, D), full),
                  pl.BlockSpec((D, LANES), full), pl.BlockSpec((1, LANES), full)],
        out_specs=[pl.BlockSpec((tm, D), row), pl.BlockSpec((tm, LANES), row), pl.BlockSpec((tm, LANES), row)],
        out_shape=[jax.ShapeDtypeStruct((N, D), F32), jax.ShapeDtypeStruct((N, LANES), F32),
                   jax.ShapeDtypeStruct((N, LANES), jnp.int32)],
        compiler_params=_cparams(("parallel", "parallel")),
        name="conv_layer",
    )(x2d, x2d, x2d, w_in, conv_w, w_out, lng, lnb, rw, rb)


def _moe_kernel(blk_e_ref, nvalid_ref, tok_ref, tokn_ref, slot_ref,
                x_hbm, w1g_ref, w1l_ref, b1g_ref, b1l_ref, w2_ref, b2_ref,
                y_hbm, xbuf, ybuf, gsem, ssem, *, bm, nb):
    b = pl.program_id(0)
    cur = b % 2

    def gather_start(idx_ref, slot):
        def body(r, c):
            t = idx_ref[0, 0, r]
            pltpu.make_async_copy(x_hbm.at[pl.ds(t, 1)], xbuf.at[slot, pl.ds(r, 1)], gsem.at[slot]).start()
            return c
        lax.fori_loop(0, bm, body, 0, unroll=8)

    def gather_wait(slot):
        pltpu.make_async_copy(x_hbm.at[pl.ds(0, bm)], xbuf.at[slot], gsem.at[slot]).wait()

    def scatter_row(slot, r):
        s = slot_ref[0, 0, r]
        pltpu.make_async_copy(ybuf.at[slot, pl.ds(r, 1)], y_hbm.at[pl.ds(s, 1)], ssem.at[slot]).start()

    def scatter_start(slot, n):
        def body8(g, c):
            for u in range(SCATTER_UNROLL):
                scatter_row(slot, g * SCATTER_UNROLL + u)
            return c
        n8 = n // SCATTER_UNROLL
        lax.fori_loop(0, n8, body8, 0)

        def body1(r, c):
            scatter_row(slot, r)
            return c
        lax.fori_loop(n8 * SCATTER_UNROLL, n, body1, 0)

    def scatter_wait(slot, n):
        n8 = pl.multiple_of((n // SCATTER_UNROLL) * SCATTER_UNROLL, SCATTER_UNROLL)

        @pl.when(n8 > 0)
        def _():
            pltpu.make_async_copy(ybuf.at[slot, pl.ds(0, n8)], y_hbm.at[pl.ds(0, n8)], ssem.at[slot]).wait()

        def body1(r, c):
            pltpu.make_async_copy(ybuf.at[slot, pl.ds(0, 1)], y_hbm.at[pl.ds(0, 1)], ssem.at[slot]).wait()
            return c
        lax.fori_loop(n8, n, body1, 0)

    nv_cur = nvalid_ref[b]
    nv_prev = nvalid_ref[jnp.maximum(b - 1, 0)]
    nv_prev2 = nvalid_ref[jnp.maximum(b - 2, 0)]
    active = nv_cur > 0
    next_active = jnp.logical_and(b + 1 < nb, nvalid_ref[jnp.minimum(b + 1, nb - 1)] > 0)

    @pl.when(jnp.logical_and(active, b == 0))
    def _():
        gather_start(tok_ref, 0)

    @pl.when(jnp.logical_and(active, next_active))
    def _():
        gather_start(tokn_ref, 1 - cur)

    @pl.when(active)
    def _():
        gather_wait(cur)
        xb = xbuf[cur].astype(BF16)
        hg = jnp.dot(xb, w1g_ref[0], preferred_element_type=F32) + b1g_ref[0]
        hl = jnp.dot(xb, w1l_ref[0], preferred_element_type=F32) + b1l_ref[0]
        glu = jnp.minimum(hg, SWIGLU_LIMIT)
        lin = jnp.clip(hl, -SWIGLU_LIMIT, SWIGLU_LIMIT)
        act = glu * jax.nn.sigmoid(SWIGLU_ALPHA * glu) * (lin + 1.0)
        y = jnp.dot(act.astype(BF16), w2_ref[0], preferred_element_type=F32) + b2_ref[0]

        @pl.when(b >= 2)
        def _():
            scatter_wait(cur, nv_prev2)

        ybuf[cur] = y
        scatter_start(cur, nv_cur)

        @pl.when(jnp.logical_not(next_active))
        def _():
            @pl.when(b >= 1)
            def _():
                scatter_wait(1 - cur, nv_prev)
            scatter_wait(cur, nv_cur)


def _moe_experts(blk_e, nvalid, row_tok, row_slot, x1, w1g, w1l, b1g, b1l, w2, b2, bm):
    N, D = x1.shape
    nb = blk_e.shape[0]
    F = w1g.shape[-1]
    smem_blk = lambda f: pl.BlockSpec((1, 1, bm), f, memory_space=pltpu.SMEM)
    wspec = lambda r, c: pl.BlockSpec((1, r, c), lambda b, be, nv: (be[b], 0, 0))
    grid_spec = pltpu.PrefetchScalarGridSpec(
        num_scalar_prefetch=2,
        grid=(nb,),
        in_specs=[
            smem_blk(lambda b, be, nv: (b, 0, 0)),
            smem_blk(lambda b, be, nv: (jnp.minimum(b + 1, nb - 1), 0, 0)),
            smem_blk(lambda b, be, nv: (b, 0, 0)),
            pl.BlockSpec(memory_space=pl.ANY),
            wspec(D, F), wspec(D, F), wspec(1, F), wspec(1, F), wspec(F, D), wspec(1, D),
        ],
        out_specs=pl.BlockSpec(memory_space=pl.ANY),
        scratch_shapes=[pltpu.VMEM((2, bm, D), F32), pltpu.VMEM((2, bm, D), F32),
                        pltpu.SemaphoreType.DMA((2,)), pltpu.SemaphoreType.DMA((2,))],
    )
    return pl.pallas_call(
        functools.partial(_moe_kernel, bm=bm, nb=nb),
        grid_spec=grid_spec,
        out_shape=jax.ShapeDtypeStruct((N * TOP_K, D), F32),
        compiler_params=_cparams(("arbitrary",)),
        name="moe_experts",
    )(blk_e, nvalid, row_tok, row_tok, row_slot, x1, w1g, w1l, b1g, b1l, w2, b2)


def _combine_kernel(y_ref, gate_ref, x_ref, lng_ref, lnb_ref, o_ref, *, alpha):
    D = x_ref.shape[-1]
    g = gate_ref[...]
    acc = alpha * x_ref[...]
    for k in range(TOP_K):
        acc = acc + g[:, k:k + 1] * y_ref[:, k * D:(k + 1) * D]
    o_ref[...] = _layer_norm(acc, lng_ref[...], lnb_ref[...])


def _combine_ln(y4, gates, x1, lng, lnb, alpha, tm):
    N, D = x1.shape
    row = lambda i: (i, 0)
    full = lambda i: (0, 0)
    return pl.pallas_call(
        functools.partial(_combine_kernel, alpha=alpha),
        grid=(N // tm,),
        in_specs=[pl.BlockSpec((tm, TOP_K * D), row), pl.BlockSpec((tm, LANES), row), pl.BlockSpec((tm, D), row),
                  pl.BlockSpec((1, D), full), pl.BlockSpec((1, D), full)],
        out_specs=pl.BlockSpec((tm, D), row),
        out_shape=jax.ShapeDtypeStruct((N, D), F32),
        compiler_params=_cparams(("parallel",)),
        name="combine_ln",
    )(y4, gates, x1, lng, lnb)


def _routing_tables(experts, bm):
    N = experts.shape[0]
    NK = N * TOP_K
    n_pad = N_EXPERTS * (bm - 1)
    nb = (NK + n_pad + bm - 1) // bm
    P = nb * bm
    flat_e = experts.reshape(-1)
    counts = jnp.sum((flat_e[:, None] == jnp.arange(N_EXPERTS, dtype=jnp.int32)[None, :]).astype(jnp.int32), axis=0)
    padded = (counts + bm - 1) // bm * bm
    need = padded - counts
    p_end = jnp.cumsum(padded)
    p_start = p_end - padded
    pad_e = jnp.repeat(jnp.arange(N_EXPERTS, dtype=jnp.int32), bm - 1)
    pad_j = jnp.tile(jnp.arange(bm - 1, dtype=jnp.int32), N_EXPERTS)
    pad_key = jnp.where(pad_j < need[pad_e], pad_e, N_EXPERTS)
    tail = P - NK - n_pad
    keys = jnp.concatenate([flat_e, pad_key, jnp.full((tail,), N_EXPERTS, jnp.int32)])
    vals = jnp.concatenate([jnp.arange(NK, dtype=jnp.int32), jnp.full((n_pad + tail,), -1, jnp.int32)])
    _, sv = lax.sort((keys, vals), num_keys=1, is_stable=True)
    row_tok = jnp.where(sv >= 0, sv // TOP_K, 0).reshape(nb, 1, bm)
    row_slot = jnp.maximum(sv, 0).reshape(nb, 1, bm)
    blk_start = jnp.arange(nb, dtype=jnp.int32) * bm
    blk_e = jnp.minimum(jnp.searchsorted(p_end, blk_start, side='right'), N_EXPERTS - 1).astype(jnp.int32)
    nvalid = jnp.clip(counts[blk_e] - (blk_start - p_start[blk_e]), 0, bm).astype(jnp.int32)
    return blk_e, nvalid, row_tok, row_slot


def _moe_layer(x1, gates, experts, w1g, w1l, b1g, b1l, w2, b2, lng, lnb, alpha, bm, tm):
    N, D = x1.shape
    blk_e, nvalid, row_tok, row_slot = _routing_tables(experts[:, :TOP_K], bm)
    y4 = _moe_experts(blk_e, nvalid, row_tok, row_slot, x1, w1g, w1l, b1g, b1l, w2, b2, bm)
    y4 = y4.reshape(N, TOP_K * D)
    return _combine_ln(y4, gates, x1, lng, lnb, alpha, tm)


def _rel_bucket(rel):
    nb = NUM_BUCKETS // 2
    max_exact = nb // 2
    ret = jnp.where(rel > 0, nb, 0)
    n = jnp.abs(rel)
    nf = jnp.maximum(n, 1).astype(F32)
    large = max_exact + (jnp.log(nf / max_exact) / math.log(MAX_DISTANCE / max_exact)
                         * (nb - max_exact)).astype(jnp.int32)
    large = jnp.minimum(large, nb - 1)
    return ret + jnp.where(n < max_exact, n, large)


def _bias_tiles(rel_bias, T):
    assert T >= MAX_DISTANCE
    kk = jnp.arange(T, dtype=jnp.int32)[:, None]
    qq = jnp.arange(T, dtype=jnp.int32)[None, :]
    tiles = []
    for d in range(-2, 3):
        bucket = _rel_bucket(d * T + kk - qq)
        tiles.append(rel_bias[bucket].astype(F32))
    bt = jnp.stack(tiles, axis=0)
    return jnp.transpose(bt, (3, 0, 1, 2)) * LOG2E


def _tiles(B, S):
    T = min(512, S)
    tm = min(512, S)
    bm = 512
    return T, tm, bm


def kernel(x, rel_bias, attn_w_in, attn_lambda, attn_subln, attn_w_out, conv_w_in, conv_w, conv_w_out,
           router_w, router_b, w1, b1, w2, b2, ln_g, ln_b):
    B, S, D = x.shape
    depth = ln_g.shape[0]
    N = B * S
    T, tm, bm = _tiles(B, S)
    alpha = (2.0 * depth) ** 0.25
    x2d = x.reshape(N, D)
    bias_tiles = _bias_tiles(rel_bias, T)
    E = router_w.shape[-1]
    rw_pad = jnp.pad(router_w, ((0, 0), (0, 0), (0, LANES - E)))
    rb_pad = jnp.pad(router_b, ((0, 0), (0, LANES - E)), constant_values=NEG_BIG)[:, None, :]
    for i in range(depth):
        j = i // 2
        lng = ln_g[i][:, None, :]
        lnb = ln_b[i][:, None, :]
        if i % 2 == 0:
            lambda_init = 0.8 - 0.6 * math.exp(-0.3 * i)
            w_in = attn_w_in[j]
            wq = w_in[:, :D].astype(BF16)
            wk = w_in[:, D:2 * D].astype(BF16)
            wvt = w_in[:, 2 * D:].T.astype(BF16)
            lf = attn_lambda[j].astype(F32)
            lam = (jnp.exp(jnp.sum(lf[0] * lf[1])) - jnp.exp(jnp.sum(lf[2] * lf[3])) + lambda_init).reshape(1)
            g = (attn_subln[j] * (1.0 - lambda_init)).reshape(1, V_DIM)
            q, k, vt = _attn_proj(x2d, wq, wk, wvt, B, S, tm)
            o = _flash_attention(lam, q, k, vt, bias_tiles, g, B, S, T)
            x1, gates, experts = _outproj_ln_router(o, attn_w_out[j].astype(BF16), x2d, lng[0], lnb[0],
                                                    rw_pad[i], rb_pad[i], alpha, tm)
        else:
            x1, gates, experts = _conv_layer(x2d, conv_w_in[j].astype(BF16), conv_w[j], conv_w_out[j].astype(BF16),
                                             lng[0], lnb[0], rw_pad[i], rb_pad[i], alpha, B, S, tm)
        w1g = w1[i][:, :, 0::2].astype(BF16)
        w1l = w1[i][:, :, 1::2].astype(BF16)
        b1g = b1[i][:, None, 0::2]
        b1l = b1[i][:, None, 1::2]
        x2d = _moe_layer(x1, gates, experts, w1g, w1l, b1g, b1l, w2[i].astype(BF16), b2[i][:, None, :],
                         lng[1], lnb[1], alpha, bm, tm)
    return x2d.reshape(B, S, D)
```

```python
import functools
import math

import jax
import jax.numpy as jnp
from jax import lax
from jax.experimental import pallas as pl
from jax.experimental.pallas import tpu as pltpu

N_HEADS = 8
HEAD_DIM = 64
V_DIM = 2 * HEAD_DIM
NUM_BUCKETS = 32
MAX_DISTANCE = 128
CONV_WIDTH = 3
N_EXPERTS = 32
TOP_K = 4
SWIGLU_ALPHA = 1.702
SWIGLU_LIMIT = 7.0
LN_EPS = 1e-5
LOG2E = 1.4426950408889634
NEG_BIG = -1e30
LANES = 128
SCATTER_UNROLL = 8
VMEM_LIMIT = 56 * 1024 * 1024

F32 = jnp.float32
BF16 = jnp.bfloat16


def _cparams(sem, vmem=VMEM_LIMIT):
    return pltpu.CompilerParams(dimension_semantics=sem, vmem_limit_bytes=vmem)


def _layer_norm(t, g, b):
    mu = jnp.mean(t, axis=-1, keepdims=True)
    c = t - mu
    var = jnp.mean(c * c, axis=-1, keepdims=True)
    return c * lax.rsqrt(var + LN_EPS) * g + b


def _router_topk(x1, rw, rb):
    logits = jnp.dot(x1, rw, preferred_element_type=F32, precision=lax.Precision.HIGHEST) + rb
    tm = logits.shape[0]
    lane = lax.broadcasted_iota(jnp.int32, (tm, LANES), 1)
    vals, idxs = [], []
    lg = logits
    for _ in range(TOP_K):
        mx = jnp.max(lg, axis=-1, keepdims=True)
        idx = jnp.min(jnp.where(lg == mx, lane, LANES), axis=-1, keepdims=True)
        vals.append(mx)
        idxs.append(idx)
        lg = jnp.where(lane == idx, -jnp.inf, lg)
    es = [jnp.exp(v - vals[0]) for v in vals]
    denom = es[0] + es[1] + es[2] + es[3]
    gates = jnp.zeros((tm, LANES), F32)
    experts = jnp.zeros((tm, LANES), jnp.int32)
    for k in range(TOP_K):
        gates = jnp.where(lane == k, es[k] / denom, gates)
        experts = jnp.where(lane == k, idxs[k], experts)
    return gates, experts


def _ln_router_epilogue(pre, lng_ref, lnb_ref, rw_ref, rb_ref, x1_ref, gate_ref, exp_ref):
    x1 = _layer_norm(pre, lng_ref[...], lnb_ref[...])
    x1_ref[...] = x1
    gates, experts = _router_topk(x1, rw_ref[...], rb_ref[...])
    gate_ref[...] = gates
    exp_ref[...] = experts


def _attn_proj_kernel(x_ref, wq_ref, wk_ref, wvt_ref, q_ref, k_ref, vt_ref, *, q_scale):
    xb = x_ref[...].astype(BF16)
    q = jnp.dot(xb, wq_ref[...], preferred_element_type=F32) * q_scale
    q_ref[...] = q.astype(BF16)
    k_ref[...] = jnp.dot(xb, wk_ref[...], preferred_element_type=F32).astype(BF16)
    vt = lax.dot_general(wvt_ref[...], xb, (((1,), (1,)), ((), ())), preferred_element_type=F32)
    vt_ref[0] = vt.astype(BF16)


def _attn_proj(x2d, wq, wk, wvt, B, S, tm):
    N, D = x2d.shape
    ns = S // tm
    row = lambda b, i: (b * ns + i, 0)
    full = lambda b, i: (0, 0)
    return pl.pallas_call(
        functools.partial(_attn_proj_kernel, q_scale=LOG2E * HEAD_DIM ** -0.5),
        grid=(B, ns),
        in_specs=[pl.BlockSpec((tm, D), row),
                  pl.BlockSpec((D, D), full), pl.BlockSpec((D, D), full), pl.BlockSpec((D, D), full)],
        out_specs=[pl.BlockSpec((tm, D), row), pl.BlockSpec((tm, D), row),
                   pl.BlockSpec((1, D, tm), lambda b, i: (b, 0, i))],
        out_shape=[jax.ShapeDtypeStruct((N, D), BF16), jax.ShapeDtypeStruct((N, D), BF16),
                   jax.ShapeDtypeStruct((B, D, S), BF16)],
        compiler_params=_cparams(("parallel", "parallel")),
        name="attn_proj",
    )(x2d, wq, wk, wvt)


def _flash_kernel(lam_ref, q_ref, k_ref, vt_ref, bias_ref, g_ref, o_ref,
                  m_ref, l_ref, acc_ref, *, nk):
    i = pl.program_id(2)
    j = pl.program_id(3)

    @pl.when(j == 0)
    def _():
        m_ref[...] = jnp.full_like(m_ref, NEG_BIG)
        l_ref[...] = jnp.zeros_like(l_ref)
        acc_ref[...] = jnp.zeros_like(acc_ref)

    d = jnp.clip(j - i, -2, 2) + 2
    bias = bias_ref[0, d]
    q = q_ref[...]
    k = k_ref[...]
    vt = vt_ref[0]
    for mp in range(2):
        km = k[:, mp * HEAD_DIM:(mp + 1) * HEAD_DIM]
        qm = q[:, mp * HEAD_DIM:(mp + 1) * HEAD_DIM]
        s = lax.dot_general(km, qm, (((1,), (1,)), ((), ())), preferred_element_type=F32) + bias
        m_old = m_ref[mp]
        m_new = jnp.maximum(m_old, jnp.max(s, axis=0, keepdims=True))
        alpha = jnp.exp2(m_old - m_new)
        p = jnp.exp2(s - m_new)
        l_ref[mp] = alpha * l_ref[mp] + jnp.sum(p, axis=0, keepdims=True)
        acc_ref[mp] = alpha * acc_ref[mp] + jnp.dot(vt, p.astype(BF16), preferred_element_type=F32)
        m_ref[mp] = m_new

    @pl.when(j == nk - 1)
    def _():
        lam = lam_ref[0]
        o = acc_ref[0] / l_ref[0] - lam * (acc_ref[1] / l_ref[1])
        ms = jnp.mean(o * o, axis=0, keepdims=True)
        o = o * lax.rsqrt(ms + LN_EPS)
        o_ref[...] = (o.T * g_ref[...]).astype(BF16)


def _flash_attention(lam, q, k, vt, bias_tiles, g, B, S, T):
    N, D = q.shape
    nq = S // T
    nk = S // T
    grid_spec = pltpu.PrefetchScalarGridSpec(
        num_scalar_prefetch=1,
        grid=(N_HEADS, B, nq, nk),
        in_specs=[
            pl.BlockSpec((T, V_DIM), lambda h, b, i, j, lam: (b * nq + i, h)),
            pl.BlockSpec((T, V_DIM), lambda h, b, i, j, lam: (b * nk + j, h)),
            pl.BlockSpec((1, V_DIM, T), lambda h, b, i, j, lam: (b, h, j)),
            pl.BlockSpec((1, 5, T, T), lambda h, b, i, j, lam: (h, 0, 0, 0)),
            pl.BlockSpec((1, V_DIM), lambda h, b, i, j, lam: (0, 0)),
        ],
        out_specs=pl.BlockSpec((T, V_DIM), lambda h, b, i, j, lam: (b * nq + i, h)),
        scratch_shapes=[pltpu.VMEM((2, 1, T), F32), pltpu.VMEM((2, 1, T), F32),
                        pltpu.VMEM((2, V_DIM, T), F32)],
    )
    return pl.pallas_call(
        functools.partial(_flash_kernel, nk=nk),
        grid_spec=grid_spec,
        out_shape=jax.ShapeDtypeStruct((N, D), BF16),
        compiler_params=_cparams(("parallel", "parallel", "parallel", "arbitrary")),
        name="flash_diff_attn",
    )(lam, q, k, vt, bias_tiles, g)


def _outproj_kernel(o_ref, w_ref, x_ref, lng_ref, lnb_ref, rw_ref, rb_ref,
                    x1_ref, gate_ref, exp_ref, *, alpha):
    h = jnp.dot(o_ref[...], w_ref[...], preferred_element_type=F32)
    pre = alpha * x_ref[...] + h
    _ln_router_epilogue(pre, lng_ref, lnb_ref, rw_ref, rb_ref, x1_ref, gate_ref, exp_ref)


def _outproj_ln_router(o, w_out, x2d, lng, lnb, rw, rb, alpha, tm):
    N, D = x2d.shape
    row = lambda i: (i, 0)
    full = lambda i: (0, 0)
    return pl.pallas_call(
        functools.partial(_outproj_kernel, alpha=alpha),
        grid=(N // tm,),
        in_specs=[pl.BlockSpec((tm, D), row), pl.BlockSpec((D, D), full), pl.BlockSpec((tm, D), row),
                  pl.BlockSpec((1, D), full), pl.BlockSpec((1, D), full),
                  pl.BlockSpec((D, LANES), full), pl.BlockSpec((1, LANES), full)],
        out_specs=[pl.BlockSpec((tm, D), row), pl.BlockSpec((tm, LANES), row), pl.BlockSpec((tm, LANES), row)],
        out_shape=[jax.ShapeDtypeStruct((N, D), F32), jax.ShapeDtypeStruct((N, LANES), F32),
                   jax.ShapeDtypeStruct((N, LANES), jnp.int32)],
        compiler_params=_cparams(("parallel",)),
        name="outproj_ln_router",
    )(o, w_out, x2d, lng, lnb, rw, rb)


def _conv_kernel(x_ref, xp_ref, xn_ref, win_ref, cw_ref, wout_ref, lng_ref, lnb_ref, rw_ref, rb_ref,
                 x1_ref, gate_ref, exp_ref, *, alpha, ns):
    i = pl.program_id(1)
    x = x_ref[...]
    tm, D = x.shape
    h = jnp.dot(x.astype(BF16), win_ref[...], preferred_element_type=F32)
    gb = h[:, :D]
    z = h[:, D:2 * D] * h[:, 2 * D:]
    xh = jnp.concatenate([xp_ref[...], xn_ref[...]], axis=0).astype(BF16)
    hh = jnp.dot(xh, win_ref[:, D:], preferred_element_type=F32)
    zh = hh[:, :D] * hh[:, D:]
    z_before = jnp.where(i > 0, zh[7:8, :], 0.0)
    z_after = jnp.where(i < ns - 1, zh[8:9, :], 0.0)
    rowid = lax.broadcasted_iota(jnp.int32, (tm, 1), 0)
    z_m = jnp.where(rowid == 0, z_before, pltpu.roll(z, 1, 0))
    z_p = jnp.where(rowid == tm - 1, z_after, pltpu.roll(z, tm - 1, 0))
    cw = cw_ref[...]
    zc = cw[0:1, :] * z_m + cw[1:2, :] * z + cw[2:3, :] * z_p
    gin = (gb * zc).astype(BF16)
    pre = alpha * x + jnp.dot(gin, wout_ref[...], preferred_element_type=F32)
    _ln_router_epilogue(pre, lng_ref, lnb_ref, rw_ref, rb_ref, x1_ref, gate_ref, exp_ref)


def _conv_layer(x2d, w_in, conv_w, w_out, lng, lnb, rw, rb, alpha, B, S, tm):
    N, D = x2d.shape
    ns = S // tm
    hb = tm // 8
    nb8 = N // 8
    row = lambda b, i: (b * ns + i, 0)
    full = lambda b, i: (0, 0)
    prev8 = lambda b, i: (jnp.maximum((b * ns + i) * hb - 1, 0), 0)
    next8 = lambda b, i: (jnp.minimum((b * ns + i + 1) * hb, nb8 - 1), 0)
    return pl.pallas_call(
        functools.partial(_conv_kernel, alpha=alpha, ns=ns),
        grid=(B, ns),
        in_specs=[pl.BlockSpec((tm, D), row), pl.BlockSpec((8, D), prev8), pl.BlockSpec((8, D), next8),
                  pl.BlockSpec((D, 3 * D), full), pl.BlockSpec((CONV_WIDTH, D), full),
                  pl.BlockSpec((D, D), full), pl.BlockSpec((1, D), full), pl.BlockSpec((1, D), full),
                  pl.BlockSpec((D, LANES), full), pl.BlockSpec((1, LANES), full)],
        out_specs=[pl.BlockSpec((tm, D), row), pl.BlockSpec((tm, LANES), row), pl.BlockSpec((tm, LANES), row)],
        out_shape=[jax.ShapeDtypeStruct((N, D), F32), jax.ShapeDtypeStruct((N, LANES), F32),
                   jax.ShapeDtypeStruct((N, LANES), jnp.int32)],
        compiler_params=_cparams(("parallel", "parallel")),
        name="conv_layer",
    )(x2d, x2d, x2d, w_in, conv_w, w_out, lng, lnb, rw, rb)


def _moe_kernel(blk_e_ref, nvalid_ref, tok_ref, tokn_ref, slot_ref,
                x_hbm, w1g_ref, w1l_ref, b1g_ref, b1l_ref, w2_ref, b2_ref,
                y_hbm, xbuf, ybuf, gsem, ssem, *, bm, nb):
    b = pl.program_id(0)
    cur = b % 2

    def gather_start(idx_ref, slot):
        def body(r, c):
            t = idx_ref[0, 0, r]
            pltpu.make_async_copy(x_hbm.at[pl.ds(t, 1)], xbuf.at[slot, pl.ds(r, 1)], gsem.at[slot]).start()
            return c
        lax.fori_loop(0, bm, body, 0, unroll=8)

    def gather_wait(slot):
        pltpu.make_async_copy(x_hbm.at[pl.ds(0, bm)], xbuf.at[slot], gsem.at[slot]).wait()

    def scatter_row(slot, r):
        s = slot_ref[0, 0, r]
        pltpu.make_async_copy(ybuf.at[slot, pl.ds(r, 1)], y_hbm.at[pl.ds(s, 1)], ssem.at[slot]).start()

    def scatter_start(slot, n):
        def body8(g, c):
            for u in range(SCATTER_UNROLL):
                scatter_row(slot, g * SCATTER_UNROLL + u)
            return c
        n8 = n // SCATTER_UNROLL
        lax.fori_loop(0, n8, body8, 0)

        def body1(r, c):
            scatter_row(slot, r)
            return c
        lax.fori_loop(n8 * SCATTER_UNROLL, n, body1, 0)

    def scatter_wait(slot, n):
        n8 = pl.multiple_of((n // SCATTER_UNROLL) * SCATTER_UNROLL, SCATTER_UNROLL)

        @pl.when(n8 > 0)
        def _():
            pltpu.make_async_copy(ybuf.at[slot, pl.ds(0, n8)], y_hbm.at[pl.ds(0, n8)], ssem.at[slot]).wait()

        def body1(r, c):
            pltpu.make_async_copy(ybuf.at[slot, pl.ds(0, 1)], y_hbm.at[pl.ds(0, 1)], ssem.at[slot]).wait()
            return c
        lax.fori_loop(n8, n, body1, 0)

    nv_cur = nvalid_ref[b]
    nv_prev = nvalid_ref[jnp.maximum(b - 1, 0)]
    nv_prev2 = nvalid_ref[jnp.maximum(b - 2, 0)]
    active = nv_cur > 0
    next_active = jnp.logical_and(b + 1 < nb, nvalid_ref[jnp.minimum(b + 1, nb - 1)] > 0)

    @pl.when(jnp.logical_and(active, b == 0))
    def _():
        gather_start(tok_ref, 0)

    @pl.when(jnp.logical_and(active, next_active))
    def _():
        gather_start(tokn_ref, 1 - cur)

    @pl.when(active)
    def _():
        gather_wait(cur)
        xb = xbuf[cur].astype(BF16)
        hg = jnp.dot(xb, w1g_ref[0], preferred_element_type=F32) + b1g_ref[0]
        hl = jnp.dot(xb, w1l_ref[0], preferred_element_type=F32) + b1l_ref[0]
        glu = jnp.minimum(hg, SWIGLU_LIMIT)
        lin = jnp.clip(hl, -SWIGLU_LIMIT, SWIGLU_LIMIT)
        act = glu * jax.nn.sigmoid(SWIGLU_ALPHA * glu) * (lin + 1.0)
        y = jnp.dot(act.astype(BF16), w2_ref[0], preferred_element_type=F32) + b2_ref[0]

        @pl.when(b >= 2)
        def _():
            scatter_wait(cur, nv_prev2)

        ybuf[cur] = y
        scatter_start(cur, nv_cur)

        @pl.when(jnp.logical_not(next_active))
        def _():
            @pl.when(b >= 1)
            def _():
                scatter_wait(1 - cur, nv_prev)
            scatter_wait(cur, nv_cur)


def _moe_experts(blk_e, nvalid, row_tok, row_slot, x1, w1g, w1l, b1g, b1l, w2, b2, bm):
    N, D = x1.shape
    nb = blk_e.shape[0]
    F = w1g.shape[-1]
    smem_blk = lambda f: pl.BlockSpec((1, 1, bm), f, memory_space=pltpu.SMEM)
    wspec = lambda r, c: pl.BlockSpec((1, r, c), lambda b, be, nv: (be[b], 0, 0))
    grid_spec = pltpu.PrefetchScalarGridSpec(
        num_scalar_prefetch=2,
        grid=(nb,),
        in_specs=[
            smem_blk(lambda b, be, nv: (b, 0, 0)),
            smem_blk(lambda b, be, nv: (jnp.minimum(b + 1, nb - 1), 0, 0)),
            smem_blk(lambda b, be, nv: (b, 0, 0)),
            pl.BlockSpec(memory_space=pl.ANY),
            wspec(D, F), wspec(D, F), wspec(1, F), wspec(1, F), wspec(F, D), wspec(1, D),
        ],
        out_specs=pl.BlockSpec(memory_space=pl.ANY),
        scratch_shapes=[pltpu.VMEM((2, bm, D), F32), pltpu.VMEM((2, bm, D), F32),
                        pltpu.SemaphoreType.DMA((2,)), pltpu.SemaphoreType.DMA((2,))],
    )
    return pl.pallas_call(
        functools.partial(_moe_kernel, bm=bm, nb=nb),
        grid_spec=grid_spec,
        out_shape=jax.ShapeDtypeStruct((N * TOP_K, D), F32),
        compiler_params=_cparams(("arbitrary",)),
        name="moe_experts",
    )(blk_e, nvalid, row_tok, row_tok, row_slot, x1, w1g, w1l, b1g, b1l, w2, b2)


def _split_w1_kernel(w_ref, pg_ref, pl_ref, g_ref, l_ref):
    wb = w_ref[...].astype(BF16)
    g_ref[...] = jnp.dot(wb, pg_ref[...], preferred_element_type=F32).astype(BF16)
    l_ref[...] = jnp.dot(wb, pl_ref[...], preferred_element_type=F32).astype(BF16)


def _split_w1(w1, tr):
    lead = w1.shape[:-2]
    D, F2 = w1.shape[-2:]
    F = F2 // 2
    w2d = w1.reshape(-1, F2)
    R = w2d.shape[0]
    col = lax.broadcasted_iota(jnp.int32, (F2, F), 0)
    out = lax.broadcasted_iota(jnp.int32, (F2, F), 1)
    pg = (col == 2 * out).astype(BF16)
    pl_ = (col == 2 * out + 1).astype(BF16)
    row = lambda i: (i, 0)
    full = lambda i: (0, 0)
    wg, wl = pl.pallas_call(
        _split_w1_kernel,
        grid=(R // tr,),
        in_specs=[pl.BlockSpec((tr, F2), row), pl.BlockSpec((F2, F), full), pl.BlockSpec((F2, F), full)],
        out_specs=[pl.BlockSpec((tr, F), row), pl.BlockSpec((tr, F), row)],
        out_shape=[jax.ShapeDtypeStruct((R, F), BF16), jax.ShapeDtypeStruct((R, F), BF16)],
        compiler_params=_cparams(("parallel",)),
        name="split_w1",
    )(w2d, pg, pl_)
    return wg.reshape(lead + (D, F)), wl.reshape(lead + (D, F))


def _combine_kernel(y_ref, gate_ref, x_ref, lng_ref, lnb_ref, o_ref, *, alpha):
    D = x_ref.shape[-1]
    g = gate_ref[...]
    acc = alpha * x_ref[...]
    for k in range(TOP_K):
        acc = acc + g[:, k:k + 1] * y_ref[k]
    o_ref[...] = _layer_norm(acc, lng_ref[...], lnb_ref[...])


def _combine_ln(y4, gates, x1, lng, lnb, alpha, tm):
    N, D = x1.shape
    row = lambda i: (i, 0)
    full = lambda i: (0, 0)
    return pl.pallas_call(
        functools.partial(_combine_kernel, alpha=alpha),
        grid=(N // tm,),
        in_specs=[pl.BlockSpec((TOP_K, tm, D), lambda i: (0, i, 0)),
                  pl.BlockSpec((tm, LANES), row), pl.BlockSpec((tm, D), row),
                  pl.BlockSpec((1, D), full), pl.BlockSpec((1, D), full)],
        out_specs=pl.BlockSpec((tm, D), row),
        out_shape=jax.ShapeDtypeStruct((N, D), F32),
        compiler_params=_cparams(("parallel",)),
        name="combine_ln",
    )(y4, gates, x1, lng, lnb)


def _routing_tables(experts, bm):
    N = experts.shape[0]
    NK = N * TOP_K
    n_pad = N_EXPERTS * (bm - 1)
    nb = (NK + n_pad + bm - 1) // bm
    P = nb * bm
    flat_e = experts.reshape(-1)
    counts = jnp.sum((flat_e[:, None] == jnp.arange(N_EXPERTS, dtype=jnp.int32)[None, :]).astype(jnp.int32), axis=0)
    padded = (counts + bm - 1) // bm * bm
    need = padded - counts
    p_end = jnp.cumsum(padded)
    p_start = p_end - padded
    pad_e = jnp.repeat(jnp.arange(N_EXPERTS, dtype=jnp.int32), bm - 1)
    pad_j = jnp.tile(jnp.arange(bm - 1, dtype=jnp.int32), N_EXPERTS)
    pad_key = jnp.where(pad_j < need[pad_e], pad_e, N_EXPERTS)
    tail = P - NK - n_pad
    keys = jnp.concatenate([flat_e, pad_key, jnp.full((tail,), N_EXPERTS, jnp.int32)])
    vals = jnp.concatenate([jnp.arange(NK, dtype=jnp.int32), jnp.full((n_pad + tail,), -1, jnp.int32)])
    _, sv = lax.sort((keys, vals), num_keys=1, is_stable=True)
    svc = jnp.maximum(sv, 0)
    row_tok = (svc // TOP_K).reshape(nb, 1, bm)
    row_slot = ((svc % TOP_K) * N + svc // TOP_K).reshape(nb, 1, bm)
    blk_start = jnp.arange(nb, dtype=jnp.int32) * bm
    blk_e = jnp.minimum(jnp.searchsorted(p_end, blk_start, side='right'), N_EXPERTS - 1).astype(jnp.int32)
    nvalid = jnp.clip(counts[blk_e] - (blk_start - p_start[blk_e]), 0, bm).astype(jnp.int32)
    return blk_e, nvalid, row_tok, row_slot


def _moe_layer(x1, gates, experts, w1g, w1l, b1g, b1l, w2, b2, lng, lnb, alpha, bm, tm):
    N, D = x1.shape
    blk_e, nvalid, row_tok, row_slot = _routing_tables(experts[:, :TOP_K], bm)
    y4 = _moe_experts(blk_e, nvalid, row_tok, row_slot, x1, w1g, w1l, b1g, b1l, w2, b2, bm)
    y4 = y4.reshape(TOP_K, N, D)
    return _combine_ln(y4, gates, x1, lng, lnb, alpha, tm)


def _rel_bucket(rel):
    nb = NUM_BUCKETS // 2
    max_exact = nb // 2
    ret = jnp.where(rel > 0, nb, 0)
    n = jnp.abs(rel)
    nf = jnp.maximum(n, 1).astype(F32)
    large = max_exact + (jnp.log(nf / max_exact) / math.log(MAX_DISTANCE / max_exact)
                         * (nb - max_exact)).astype(jnp.int32)
    large = jnp.minimum(large, nb - 1)
    return ret + jnp.where(n < max_exact, n, large)


def _bias_tiles(rel_bias, T):
    assert T >= MAX_DISTANCE
    kk = jnp.arange(T, dtype=jnp.int32)[:, None]
    qq = jnp.arange(T, dtype=jnp.int32)[None, :]
    dd = jnp.arange(-2, 3, dtype=jnp.int32)[:, None, None]
    bucket = _rel_bucket(dd * T + kk - qq)[None]
    rb = rel_bias.astype(F32)
    bt = jnp.zeros((rel_bias.shape[1], 5, T, T), F32)
    for bkt in range(NUM_BUCKETS):
        bt = jnp.where(bucket == bkt, rb[bkt][:, None, None, None], bt)
    return bt * LOG2E


def _tiles(B, S):
    T = min(512, S)
    tm = min(512, S)
    bm = 512
    return T, tm, bm


def kernel(x, rel_bias, attn_w_in, attn_lambda, attn_subln, attn_w_out, conv_w_in, conv_w, conv_w_out,
           router_w, router_b, w1, b1, w2, b2, ln_g, ln_b):
    B, S, D = x.shape
    depth = ln_g.shape[0]
    N = B * S
    T, tm, bm = _tiles(B, S)
    alpha = (2.0 * depth) ** 0.25
    x2d = x.reshape(N, D)
    bias_tiles = _bias_tiles(rel_bias, T)
    E = router_w.shape[-1]
    rw_pad = jnp.pad(router_w, ((0, 0), (0, 0), (0, LANES - E)))
    rb_pad = jnp.pad(router_b, ((0, 0), (0, LANES - E)), constant_values=NEG_BIG)[:, None, :]
    for i in range(depth):
        j = i // 2
        lng = ln_g[i][:, None, :]
        lnb = ln_b[i][:, None, :]
        if i % 2 == 0:
            lambda_init = 0.8 - 0.6 * math.exp(-0.3 * i)
            w_in = attn_w_in[j]
            wq = w_in[:, :D].astype(BF16)
            wk = w_in[:, D:2 * D].astype(BF16)
            wvt = w_in[:, 2 * D:].T.astype(BF16)
            lf = attn_lambda[j].astype(F32)
            lam = (jnp.exp(jnp.sum(lf[0] * lf[1])) - jnp.exp(jnp.sum(lf[2] * lf[3])) + lambda_init).reshape(1)
            g = (attn_subln[j] * (1.0 - lambda_init)).reshape(1, V_DIM)
            q, k, vt = _attn_proj(x2d, wq, wk, wvt, B, S, tm)
            o = _flash_attention(lam, q, k, vt, bias_tiles, g, B, S, T)
            x1, gates, experts = _outproj_ln_router(o, attn_w_out[j].astype(BF16), x2d, lng[0], lnb[0],
                                                    rw_pad[i], rb_pad[i], alpha, tm)
        else:
            x1, gates, experts = _conv_layer(x2d, conv_w_in[j].astype(BF16), conv_w[j], conv_w_out[j].astype(BF16),
                                             lng[0], lnb[0], rw_pad[i], rb_pad[i], alpha, B, S, tm)
        w1g, w1l = _split_w1(w1[i], tm)
        b1g = b1[i][:, None, 0::2]
        b1l = b1[i][:, None, 1::2]
        x2d = _moe_layer(x1, gates, experts, w1g, w1l, b1g, b1l, w2[i].astype(BF16), b2[i][:, None, :],
                         lng[1], lnb[1], alpha, bm, tm)
    return x2d.reshape(B, S, D)
```
